```python
import math
import jax, jax.numpy as jnp
from jax import lax
import numpy as np

D_MODEL = 1024
BATCH = 16
SEQ = 2048
DEPTH = 2

N_EVEN = (DEPTH + 1) // 2
N_ODD = DEPTH // 2
EPS = 1e-6

CONF_WIDTH = D_MODEL // 2
CONF_KERNEL = 31
LRU_WIDTH = D_MODEL // 2
LRU_HEADS = 8
LRU_HEAD_DIM = LRU_WIDTH // LRU_HEADS
LRU_CONV = 4
LRU_C = 8.0
AB_IN = 2 * CONF_WIDTH + 2 * LRU_WIDTH
AB_OUT = CONF_WIDTH + LRU_WIDTH
ML_HEADS = 8
ML_V_DIM = D_MODEL
ML_HEAD_V = ML_V_DIM // ML_HEADS
ML_HEAD_QK = ML_HEAD_V // 2
ML_QK_DIM = ML_HEADS * ML_HEAD_QK
ML_CHUNK = 64
ML_IN = 2 * ML_QK_DIM + 2 * ML_V_DIM + 2 * ML_HEADS
D_FF = int(math.ceil(8 * D_MODEL / 3 / 256)) * 256

kernel_name = 'hybrid_conformer_rglru_mlstm_block'


def _rmsnorm(x, g):
    xf = x.astype(jnp.float32)
    y = xf * lax.rsqrt(jnp.mean(xf * xf, axis=-1, keepdims=True) + EPS)
    return (y * g.astype(jnp.float32)).astype(x.dtype)


def _layernorm(x, g, b):
    xf = x.astype(jnp.float32)
    mu = jnp.mean(xf, axis=-1, keepdims=True)
    var = jnp.mean(jnp.square(xf - mu), axis=-1, keepdims=True)
    y = (xf - mu) * lax.rsqrt(var + EPS)
    return (y * g.astype(jnp.float32) + b.astype(jnp.float32)).astype(x.dtype)


def _causal_dwconv(x, w, b):
    k = w.shape[0]
    c = x.shape[-1]
    y = lax.conv_general_dilated(
        x, w[:, None, :].astype(x.dtype), window_strides=(1,), padding=[(k - 1, 0)],
        dimension_numbers=('NWC', 'WIO', 'NWC'), feature_group_count=c)
    return y + b


def _conformer_conv(u, conv_w, conv_b, ln_g, ln_b):
    val, gate = jnp.split(u, 2, axis=-1)
    y = val * jax.nn.sigmoid(gate)
    y = _causal_dwconv(y, conv_w, conv_b)
    return jax.nn.silu(_layernorm(y, ln_g, ln_b))


def _lru_combine(left, right):
    a_l, b_l = left
    a_r, b_r = right
    return a_l * a_r, a_r * b_l + b_r


def _rglru_block(u, conv_w, conv_b, w_a, b_a, w_x, b_x, lam):
    bsz, seq, _ = u.shape
    gate_in, rec_in = jnp.split(u, 2, axis=-1)
    xr = _causal_dwconv(rec_in, conv_w, conv_b)
    xh = xr.reshape(bsz, seq, LRU_HEADS, LRU_HEAD_DIM)
    r = jax.nn.sigmoid(jnp.einsum('bshd,hde->bshe', xh, w_a) + b_a)
    i = jax.nn.sigmoid(jnp.einsum('bshd,hde->bshe', xh, w_x) + b_x)
    log_base = -jax.nn.softplus(-lam.astype(jnp.float32)).reshape(LRU_HEADS, LRU_HEAD_DIM)
    log_a = LRU_C * log_base * r.astype(jnp.float32)
    a = jnp.exp(log_a)
    gated_x = jnp.sqrt(-jnp.expm1(2.0 * log_a)) * (i * xh).astype(jnp.float32)
    _, h = lax.associative_scan(_lru_combine, (a, gated_x), axis=1)
    h = h.reshape(bsz, seq, LRU_WIDTH).astype(u.dtype)
    return h * jax.nn.gelu(gate_in)


def _ab_mixer(xn, w_in, b_in, conf_conv_w, conf_conv_b, conf_ln_g, conf_ln_b,
              lru_conv_w, lru_conv_b, lru_w_a, lru_b_a, lru_w_x, lru_b_x, lru_lambda, w_out):
    u = jnp.einsum('bsd,de->bse', xn, w_in) + b_in
    u_conf, u_lru = jnp.split(u, [2 * CONF_WIDTH], axis=-1)
    y_a = _conformer_conv(u_conf, conf_conv_w, conf_conv_b, conf_ln_g, conf_ln_b)
    y_b = _rglru_block(u_lru, lru_conv_w, lru_conv_b, lru_w_a, lru_b_a, lru_w_x, lru_b_x, lru_lambda)
    y = jnp.concatenate([y_a, y_b], axis=-1)
    return jnp.einsum('bse,ed->bsd', y, w_out)


def _mlstm_chunk_step(carry, inp):
    c_state, n_state, m_state = carry
    qc, kc, vc, ic, lfc = inp
    L = qc.shape[2]
    causal = jnp.tril(jnp.ones((L, L), dtype=bool))
    b = jnp.cumsum(lfc, axis=-1)
    log_d = b[..., :, None] - b[..., None, :] + ic[..., None, :]
    log_d = jnp.where(causal, log_d, -jnp.inf)
    log_inter = b + m_state[..., None]
    m_row = jnp.maximum(log_inter, jnp.max(log_d, axis=-1))
    d_w = jnp.exp(log_d - m_row[..., None])
    inter_w = jnp.exp(log_inter - m_row)
    scores = jnp.einsum('bhjd,bhsd->bhjs', qc, kc) * d_w
    num = jnp.einsum('bhjs,bhsv->bhjv', scores, vc) + inter_w[..., None] * jnp.einsum('bhjd,bhdv->bhjv', qc, c_state)
    den = jnp.sum(scores, axis=-1) + inter_w * jnp.einsum('bhjd,bhd->bhj', qc, n_state)
    h = num / jnp.maximum(jnp.abs(den), jnp.exp(-m_row))[..., None]
    b_last = b[..., -1]
    log_w = b_last[..., None] - b + ic
    m_new = jnp.maximum(b_last + m_state, jnp.max(log_w, axis=-1))
    w_s = jnp.exp(log_w - m_new[..., None])
    decay = jnp.exp(b_last + m_state - m_new)
    c_new = decay[..., None, None] * c_state + jnp.einsum('bhsd,bhsv->bhdv', w_s[..., None] * kc, vc)
    n_new = decay[..., None] * n_state + jnp.einsum('bhs,bhsd->bhd', w_s, kc)
    return (c_new, n_new, m_new), h


def _mlstm_mixer(xn, w_in, b_in, head_g, w_out):
    bsz, seq, _ = xn.shape
    nc = seq // ML_CHUNK
    u = jnp.einsum('bsd,de->bse', xn, w_in) + b_in
    q, k, v, o, ig, fg = jnp.split(
        u, [ML_QK_DIM, 2 * ML_QK_DIM, 2 * ML_QK_DIM + ML_V_DIM,
            2 * ML_QK_DIM + 2 * ML_V_DIM, 2 * ML_QK_DIM + 2 * ML_V_DIM + ML_HEADS], axis=-1)
    f32 = jnp.float32

    def to_chunks(t, d):
        return t.astype(f32).reshape(bsz, nc, ML_CHUNK, ML_HEADS, d).transpose(1, 0, 3, 2, 4)

    def gate_chunks(t):
        return t.astype(f32).reshape(bsz, nc, ML_CHUNK, ML_HEADS).transpose(1, 0, 3, 2)

    qc = to_chunks(q, ML_HEAD_QK)
    kc = to_chunks(k, ML_HEAD_QK) * (ML_HEAD_QK ** -0.5)
    vc = to_chunks(v, ML_HEAD_V)
    ic = gate_chunks(ig)
    lfc = gate_chunks(jax.nn.log_sigmoid(fg))
    init = (jnp.zeros((bsz, ML_HEADS, ML_HEAD_QK, ML_HEAD_V), f32),
            jnp.zeros((bsz, ML_HEADS, ML_HEAD_QK), f32),
            jnp.zeros((bsz, ML_HEADS), f32))
    _, h = lax.scan(_mlstm_chunk_step, init, (qc, kc, vc, ic, lfc))
    h = h.transpose(1, 0, 3, 2, 4).reshape(bsz, seq, ML_HEADS, ML_HEAD_V)
    h = h * lax.rsqrt(jnp.mean(h * h, axis=-1, keepdims=True) + EPS)
    h = (h.reshape(bsz, seq, ML_V_DIM) * head_g.astype(f32)).astype(xn.dtype)
    h = jax.nn.sigmoid(o) * h
    return jnp.einsum('bse,ed->bsd', h, w_out)


def _swiglu(xn, w_gate, w_up, w_down):
    g = jnp.einsum('bsd,df->bsf', xn, w_gate)
    up = jnp.einsum('bsd,df->bsf', xn, w_up)
    return jnp.einsum('bsf,fd->bsd', jax.nn.silu(g) * up, w_down)


def setup_inputs(seed: int = 0) -> dict:
    key = jax.random.key(seed)
    ks = iter(jax.random.split(key, 40))

    def nrm(shape, scale):
        return jax.random.normal(next(ks), shape, jnp.float32) * scale

    def gain(shape):
        return 1.0 + nrm(shape, 0.05)

    lam_u = jax.random.uniform(next(ks), (N_EVEN, LRU_WIDTH), jnp.float32, minval=0.9, maxval=0.999)
    lam_base = lam_u ** (1.0 / LRU_C)
    lru_lambda = jnp.log(lam_base) - jnp.log1p(-lam_base)

    ml_b_in = nrm((N_ODD, ML_IN), 0.02)
    f_bias = jnp.linspace(3.0, 6.0, ML_HEADS, dtype=jnp.float32) + nrm((N_ODD, ML_HEADS), 0.1)
    ml_b_in = ml_b_in.at[:, ML_IN - ML_HEADS:].set(f_bias)

    return {
        'x': nrm((BATCH, SEQ, D_MODEL), 1.0),
        'pre_mix_g': gain((DEPTH, D_MODEL)),
        'post_mix_g': gain((DEPTH, D_MODEL)),
        'pre_ffn_g': gain((DEPTH, D_MODEL)),
        'post_ffn_g': gain((DEPTH, D_MODEL)),
        'ab_w_in': nrm((N_EVEN, D_MODEL, AB_IN), D_MODEL ** -0.5),
        'ab_b_in': nrm((N_EVEN, AB_IN), 0.02),
        'conf_conv_w': nrm((N_EVEN, CONF_KERNEL, CONF_WIDTH), CONF_KERNEL ** -0.5),
        'conf_conv_b': nrm((N_EVEN, CONF_WIDTH), 0.02),
        'conf_ln_g': gain((N_EVEN, CONF_WIDTH)),
        'conf_ln_b': nrm((N_EVEN, CONF_WIDTH), 0.02),
        'lru_conv_w': nrm((N_EVEN, LRU_CONV, LRU_WIDTH), LRU_CONV ** -0.5),
        'lru_conv_b': nrm((N_EVEN, LRU_WIDTH), 0.02),
        'lru_w_a': nrm((N_EVEN, LRU_HEADS, LRU_HEAD_DIM, LRU_HEAD_DIM), LRU_HEAD_DIM ** -0.5),
        'lru_b_a': nrm((N_EVEN, LRU_HEADS, LRU_HEAD_DIM), 0.02),
        'lru_w_x': nrm((N_EVEN, LRU_HEADS, LRU_HEAD_DIM, LRU_HEAD_DIM), LRU_HEAD_DIM ** -0.5),
        'lru_b_x': nrm((N_EVEN, LRU_HEADS, LRU_HEAD_DIM), 0.02),
        'lru_lambda': lru_lambda,
        'ab_w_out': nrm((N_EVEN, AB_OUT, D_MODEL), AB_OUT ** -0.5),
        'ml_w_in': nrm((N_ODD, D_MODEL, ML_IN), D_MODEL ** -0.5),
        'ml_b_in': ml_b_in,
        'ml_head_g': gain((N_ODD, ML_V_DIM)),
        'ml_w_out': nrm((N_ODD, ML_V_DIM, D_MODEL), ML_V_DIM ** -0.5),
        'ffn_w_gate': nrm((DEPTH, D_MODEL, D_FF), D_MODEL ** -0.5),
        'ffn_w_up': nrm((DEPTH, D_MODEL, D_FF), D_MODEL ** -0.5),
        'ffn_w_down': nrm((DEPTH, D_FF, D_MODEL), D_FF ** -0.5),
    }


def reference(x, pre_mix_g, post_mix_g, pre_ffn_g, post_ffn_g,
              ab_w_in, ab_b_in, conf_conv_w, conf_conv_b, conf_ln_g, conf_ln_b,
              lru_conv_w, lru_conv_b, lru_w_a, lru_b_a, lru_w_x, lru_b_x, lru_lambda, ab_w_out,
              ml_w_in, ml_b_in, ml_head_g, ml_w_out,
              ffn_w_gate, ffn_w_up, ffn_w_down):
    for layer in range(DEPTH):
        j = layer // 2
        h = _rmsnorm(x, pre_mix_g[layer])
        if layer % 2 == 0:
            h = _ab_mixer(h, ab_w_in[j], ab_b_in[j], conf_conv_w[j], conf_conv_b[j],
                          conf_ln_g[j], conf_ln_b[j], lru_conv_w[j], lru_conv_b[j],
                          lru_w_a[j], lru_b_a[j], lru_w_x[j], lru_b_x[j], lru_lambda[j], ab_w_out[j])
        else:
            h = _mlstm_mixer(h, ml_w_in[j], ml_b_in[j], ml_head_g[j], ml_w_out[j])
        x = x + _rmsnorm(h, post_mix_g[layer])
        h = _swiglu(_rmsnorm(x, pre_ffn_g[layer]), ffn_w_gate[layer], ffn_w_up[layer], ffn_w_down[layer])
        x = x + _rmsnorm(h, post_ffn_g[layer])
    return x
```

```python
import functools

import jax
import jax.numpy as jnp
from jax import lax
from jax.experimental import pallas as pl
from jax.experimental.pallas import tpu as pltpu

F32 = jnp.float32
BF16 = jnp.bfloat16

EPS = 1e-6
LRU_C = 8.0
CONF_KERNEL = 31
LRU_CONV = 4
LRU_HEADS = 8
ML_HEADS = 8
ML_HEAD_QK = 64
ML_HEAD_V = 128

SUBLANES = 8
LANES = 128
VMEM_LIMIT_BYTES = 56 * 1024 * 1024

CONF_HIST = 32
LRU_HIST = SUBLANES
CONV_ROWS = 32

TS_AB = 256
TS_ML = 256
TM_FFN = 512
FFN_CHUNK = 512


def _rms(x, g):
    return x * lax.rsqrt(jnp.mean(x * x, axis=-1, keepdims=True) + EPS) * g


def _dot(a, b):
    return jnp.dot(a, b, preferred_element_type=F32)


def _const_spec(shape):
    zeros = (0,) * len(shape)
    return pl.BlockSpec(shape, lambda *_: zeros)


def _scan_time(x, op, fill):
    n = x.shape[0]
    row = lax.broadcasted_iota(jnp.int32, x.shape, 0)
    s = 1
    while s < n:
        shifted = pltpu.roll(x, s, 0)
        x = op(x, jnp.where(row >= s, shifted, fill))
        s *= 2
    return x


def _ffn_kernel(x_ref, pre_g_ref, post_g_ref, wg_ref, wu_ref, wd_ref, o_ref):
    x = x_ref[...]
    h = _rms(x, pre_g_ref[...]).astype(BF16)
    d_ff = wg_ref.shape[1]
    acc = None
    for c in range(0, d_ff, FFN_CHUNK):
        w = min(FFN_CHUNK, d_ff - c)
        g = _dot(h, wg_ref[:, c:c + w])
        u = _dot(h, wu_ref[:, c:c + w])
        a = (g * jax.nn.sigmoid(g) * u).astype(BF16)
        p = _dot(a, wd_ref[c:c + w, :])
        acc = p if acc is None else acc + p
    o_ref[...] = x + _rms(acc, post_g_ref[...])


def _ffn(x2d, pre_g, post_g, wg, wu, wd):
    t, d = x2d.shape
    d_ff = wg.shape[1]
    return pl.pallas_call(
        _ffn_kernel,
        grid=(t // TM_FFN,),
        in_specs=[
            pl.BlockSpec((TM_FFN, d), lambda i: (i, 0)),
            _const_spec((1, d)), _const_spec((1, d)),
            _const_spec((d, d_ff)), _const_spec((d, d_ff)), _const_spec((d_ff, d)),
        ],
        out_specs=pl.BlockSpec((TM_FFN, d), lambda i: (i, 0)),
        out_shape=jax.ShapeDtypeStruct((t, d), F32),
        compiler_params=pltpu.CompilerParams(
            dimension_semantics=("arbitrary",), vmem_limit_bytes=VMEM_LIMIT_BYTES),
        name="swiglu_ffn",
    )(x2d, pre_g, post_g, wg, wu, wd)


def _ab_kernel(x_ref, pre_g_ref, post_g_ref, w_in_ref, b_in_ref,
               cw_ref, cb_ref, lng_ref, lnb_ref,
               lw_ref, lb_ref, wblk_ref, ba_ref, bx_ref, lam_ref, w_out_ref,
               o_ref, ybuf, rbuf, hcar, yab):
    ts = x_ref.shape[0]
    cw = ybuf.shape[1]
    lw = rbuf.shape[1]

    @pl.when(pl.program_id(1) == 0)
    def _():
        ybuf[0:CONF_HIST, :] = jnp.zeros((CONF_HIST, cw), F32)
        rbuf[0:LRU_HIST, :] = jnp.zeros((LRU_HIST, lw), F32)
        hcar[...] = jnp.zeros(hcar.shape, F32)

    x = x_ref[...]
    h = _rms(x, pre_g_ref[...]).astype(BF16)
    u = _dot(h, w_in_ref[...]) + b_in_ref[...]
    ybuf[CONF_HIST:CONF_HIST + ts, :] = u[:, 0:cw] * jax.nn.sigmoid(u[:, cw:2 * cw])
    gate_in = u[:, 2 * cw:2 * cw + lw]
    rbuf[LRU_HIST:LRU_HIST + ts, :] = u[:, 2 * cw + lw:]

    ln_g = lng_ref[...]
    ln_b = lnb_ref[...]
    for r0 in range(0, ts, CONV_ROWS):
        acc = jnp.broadcast_to(cb_ref[...], (CONV_ROWS, cw))
        for j in range(CONF_KERNEL):
            start = r0 + CONF_HIST - (CONF_KERNEL - 1) + j
            acc = acc + cw_ref[j:j + 1, :] * ybuf[start:start + CONV_ROWS, :]
        mu = jnp.mean(acc, axis=-1, keepdims=True)
        cen = acc - mu
        var = jnp.mean(cen * cen, axis=-1, keepdims=True)
        yn = cen * lax.rsqrt(var + EPS) * ln_g + ln_b
        yab[r0:r0 + CONV_ROWS, 0:cw] = (yn * jax.nn.sigmoid(yn)).astype(BF16)

    xr = jnp.broadcast_to(lb_ref[...], (ts, lw))
    for j in range(LRU_CONV):
        start = LRU_HIST - (LRU_CONV - 1) + j
        xr = xr + lw_ref[j:j + 1, :] * rbuf[start:start + ts, :]
    xb = xr.astype(BF16)
    half = lw // 2
    g0 = _dot(xb[:, 0:half], wblk_ref[0])
    g1 = _dot(xb[:, half:], wblk_ref[1])
    r = jax.nn.sigmoid(jnp.concatenate([g0[:, 0:half], g1[:, 0:half]], axis=1) + ba_ref[...])
    i = jax.nn.sigmoid(jnp.concatenate([g0[:, half:], g1[:, half:]], axis=1) + bx_ref[...])
    lam = lam_ref[...]
    log_base = -(jnp.maximum(-lam, 0.0) + jnp.log1p(jnp.exp(-jnp.abs(lam))))
    log_a = (LRU_C * log_base) * r
    a = jnp.exp(log_a)
    b = jnp.sqrt(-jnp.tanh(log_a) * (a * a + 1.0)) * (i * xr)

    row8 = lax.broadcasted_iota(jnp.int32, (ts, lw), 0) & (SUBLANES - 1)
    s = 1
    while s < SUBLANES:
        keep = row8 >= s
        a_sh = pltpu.roll(a, s, 0)
        b_sh = pltpu.roll(b, s, 0)
        b = jnp.where(keep, a * b_sh + b, b)
        a = jnp.where(keep, a * a_sh, a)
        s *= 2
    carry = hcar[...]
    hs = []
    for g in range(ts // SUBLANES):
        lo = g * SUBLANES
        hg = b[lo:lo + SUBLANES, :] + a[lo:lo + SUBLANES, :] * carry
        hs.append(hg)
        carry = jnp.broadcast_to(hg[SUBLANES - 1:SUBLANES, :], (SUBLANES, lw))
    hcar[...] = carry
    hseq = jnp.concatenate(hs, axis=0)
    yab[:, cw:cw + lw] = (hseq * jax.nn.gelu(gate_in)).astype(BF16)

    ybuf[0:CONF_HIST, :] = ybuf[ts:ts + CONF_HIST, :]
    rbuf[0:LRU_HIST, :] = rbuf[ts:ts + LRU_HIST, :]

    y = _dot(yab[...], w_out_ref[...])
    o_ref[...] = x + _rms(y, post_g_ref[...])


def _ab_mixer(x, pre_g, post_g, w_in, b_in, conf_w, conf_b, ln_g, ln_b,
              lru_w, lru_b, wblk, ba, bx, lam, w_out):
    bsz, seq, d = x.shape
    cw = conf_w.shape[1]
    lw = lru_w.shape[1]
    n_in = w_in.shape[1]
    xspec = pl.BlockSpec((None, TS_AB, d), lambda b, s: (b, s, 0))
    return pl.pallas_call(
        _ab_kernel,
        grid=(bsz, seq // TS_AB),
        in_specs=[
            xspec,
            _const_spec((1, d)), _const_spec((1, d)),
            _const_spec((d, n_in)), _const_spec((1, n_in)),
            _const_spec((CONF_KERNEL, cw)), _const_spec((1, cw)), _const_spec((1, cw)), _const_spec((1, cw)),
            _const_spec((LRU_CONV, lw)), _const_spec((1, lw)),
            _const_spec(wblk.shape), _const_spec((1, lw)), _const_spec((1, lw)), _const_spec((1, lw)),
            _const_spec((cw + lw, d)),
        ],
        out_specs=xspec,
        out_shape=jax.ShapeDtypeStruct(x.shape, F32),
        scratch_shapes=[
            pltpu.VMEM((CONF_HIST + TS_AB, cw), F32),
            pltpu.VMEM((LRU_HIST + TS_AB, lw), F32),
            pltpu.VMEM((SUBLANES, lw), F32),
            pltpu.VMEM((TS_AB, cw + lw), BF16),
        ],
        compiler_params=pltpu.CompilerParams(
            dimension_semantics=("arbitrary", "arbitrary"), vmem_limit_bytes=VMEM_LIMIT_BYTES),
        name="conformer_rglru_mixer",
    )(x, pre_g, post_g, w_in, b_in, conf_w, conf_b, ln_g, ln_b,
      lru_w, lru_b, wblk, ba, bx, lam, w_out)


def _ml_kernel(x_ref, pre_g_ref, post_g_ref, w_main_ref, b_main_ref, w_gate_ref, b_gate_ref,
               head_g_ref, w_out_ref, o_ref, c_ref, m_ref, hbuf):
    ts = x_ref.shape[0]
    qk = ML_HEADS * ML_HEAD_QK
    vd = ML_HEADS * ML_HEAD_V

    @pl.when(pl.program_id(1) == 0)
    def _():
        c_ref[...] = jnp.zeros(c_ref.shape, F32)
        m_ref[...] = jnp.zeros(m_ref.shape, F32)

    x = x_ref[...]
    h = _rms(x, pre_g_ref[...]).astype(BF16)
    u = _dot(h, w_main_ref[...]) + b_main_ref[...]
    gates = _dot(h, w_gate_ref[...]) + b_gate_ref[...]
    ig = gates[:, 0:LANES]
    fg = gates[:, LANES:2 * LANES]
    lf = jnp.minimum(fg, 0.0) - jnp.log1p(jnp.exp(-jnp.abs(fg)))

    bcum = _scan_time(lf, jnp.add, 0.0)
    rs = ig - bcum
    m_prev = m_ref[0:1, :]
    log_inter = bcum + m_prev
    m_row = jnp.maximum(log_inter, bcum + _scan_time(rs, jnp.maximum, -jnp.inf))
    c_col = bcum - m_row
    inter_w = jnp.exp(log_inter - m_row)
    e_negm = jnp.exp(-m_row)
    b_last = bcum[ts - 1:ts, :]
    log_w = b_last - bcum + ig
    m_new = jnp.maximum(b_last + m_prev, jnp.max(log_w, axis=0, keepdims=True))
    w_s = jnp.exp(log_w - m_new)
    decay = jnp.exp(b_last + m_prev - m_new)
    m_ref[...] = jnp.broadcast_to(m_new, m_ref.shape)
    rs_t = jnp.transpose(rs)

    causal = (lax.broadcasted_iota(jnp.int32, (ts, ts), 0)
              >= lax.broadcasted_iota(jnp.int32, (ts, ts), 1))
    lane_head = lax.broadcasted_iota(jnp.int32, (ts, LANES), 1) // ML_HEAD_QK
    ones_blk = jnp.ones((ts, LANES), BF16)
    head_g = head_g_ref[...]

    for hd in range(ML_HEADS):
        pair = (hd // 2) * LANES
        q_p = u[:, pair:pair + LANES].astype(BF16)
        k_m = jnp.where(lane_head == (hd % 2), u[:, qk + pair:qk + pair + LANES], 0.0) * (ML_HEAD_QK ** -0.5)
        v_h = u[:, 2 * qk + hd * ML_HEAD_V:2 * qk + (hd + 1) * ML_HEAD_V]
        o_h = u[:, 2 * qk + vd + hd * ML_HEAD_V:2 * qk + vd + (hd + 1) * ML_HEAD_V]
        v_ext = jnp.concatenate([v_h.astype(BF16), ones_blk], axis=1)

        scores = lax.dot_general(q_p, k_m.astype(BF16), (((1,), (1,)), ((), ())),
                                 preferred_element_type=F32)
        d_w = jnp.exp(jnp.where(causal, c_col[:, hd:hd + 1] + rs_t[hd:hd + 1, :], -jnp.inf))
        p = (scores * d_w).astype(BF16)
        c_h = c_ref[hd]
        res = _dot(p, v_ext) + inter_w[:, hd:hd + 1] * _dot(q_p, c_h.astype(BF16))
        num = res[:, 0:ML_HEAD_V]
        den = res[:, ML_HEAD_V:]
        hh = num / jnp.maximum(jnp.abs(den), e_negm[:, hd:hd + 1])
        hn = hh * lax.rsqrt(jnp.mean(hh * hh, axis=-1, keepdims=True) + EPS)
        hn = hn * head_g[:, hd * ML_HEAD_V:(hd + 1) * ML_HEAD_V]
        hbuf[:, hd * ML_HEAD_V:(hd + 1) * ML_HEAD_V] = (jax.nn.sigmoid(o_h) * hn).astype(BF16)

        kw = (k_m * w_s[:, hd:hd + 1]).astype(BF16)
        upd = lax.dot_general(kw, v_ext, (((0,), (0,)), ((), ())), preferred_element_type=F32)
        c_ref[hd] = decay[:, hd:hd + 1] * c_h + upd

    y = _dot(hbuf[...], w_out_ref[...])
    o_ref[...] = x + _rms(y, post_g_ref[...])


def _ml_mixer(x, pre_g, post_g, w_main, b_main, w_gate, b_gate, head_g, w_out):
    bsz, seq, d = x.shape
    n_main = w_main.shape[1]
    vd = w_out.shape[0]
    xspec = pl.BlockSpec((None, TS_ML, d), lambda b, s: (b, s, 0))
    return pl.pallas_call(
        _ml_kernel,
        grid=(bsz, seq // TS_ML),
        in_specs=[
            xspec,
            _const_spec((1, d)), _const_spec((1, d)),
            _const_spec((d, n_main)), _const_spec((1, n_main)),
            _const_spec((d, 2 * LANES)), _const_spec((1, 2 * LANES)),
            _const_spec((1, vd)), _const_spec((vd, d)),
        ],
        out_specs=xspec,
        out_shape=jax.ShapeDtypeStruct(x.shape, F32),
        scratch_shapes=[
            pltpu.VMEM((ML_HEADS, LANES, 2 * ML_HEAD_V), F32),
            pltpu.VMEM((SUBLANES, LANES), F32),
            pltpu.VMEM((TS_ML, vd), BF16),
        ],
        compiler_params=pltpu.CompilerParams(
            dimension_semantics=("arbitrary", "arbitrary"), vmem_limit_bytes=VMEM_LIMIT_BYTES),
        name="mlstm_mixer",
    )(x, pre_g, post_g, w_main, b_main, w_gate, b_gate, head_g, w_out)


def _block_diag(w):
    n, d, e = w.shape
    eye = jnp.eye(n, dtype=w.dtype)
    return jnp.einsum('hde,hg->hdge', w, eye).reshape(n * d, n * e)


def kernel(x, pre_mix_g, post_mix_g, pre_ffn_g, post_ffn_g, ab_w_in, ab_b_in, conf_conv_w, conf_conv_b, conf_ln_g, conf_ln_b, lru_conv_w, lru_conv_b, lru_w_a, lru_b_a, lru_w_x, lru_b_x, lru_lambda, ab_w_out, ml_w_in, ml_b_in, ml_head_g, ml_w_out, ffn_w_gate, ffn_w_up, ffn_w_down):
    bsz, seq, d = x.shape
    depth = pre_mix_g.shape[0]
    row = lambda v: v.reshape(1, -1).astype(F32)

    for layer in range(depth):
        j = layer // 2
        if layer % 2 == 0:
            lw = lru_conv_w.shape[-1]
            grp = LRU_HEADS // 2
            wa, wx = lru_w_a[j], lru_w_x[j]
            wblk = jnp.stack([
                jnp.concatenate([_block_diag(wa[g * grp:(g + 1) * grp]),
                                 _block_diag(wx[g * grp:(g + 1) * grp])], axis=1)
                for g in range(2)]).astype(BF16)
            x = _ab_mixer(
                x, row(pre_mix_g[layer]), row(post_mix_g[layer]),
                ab_w_in[j].astype(BF16), row(ab_b_in[j]),
                conf_conv_w[j], row(conf_conv_b[j]), row(conf_ln_g[j]), row(conf_ln_b[j]),
                lru_conv_w[j], row(lru_conv_b[j]), wblk,
                row(lru_b_a[j]), row(lru_b_x[j]), row(lru_lambda[j]),
                ab_w_out[j].astype(BF16))
            del lw
        else:
            n_main = 2 * ML_HEADS * ML_HEAD_QK + 2 * ML_HEADS * ML_HEAD_V
            w_in, b_in = ml_w_in[j], ml_b_in[j]
            w_gate = jnp.zeros((d, 2 * LANES), F32)
            w_gate = w_gate.at[:, 0:ML_HEADS].set(w_in[:, n_main:n_main + ML_HEADS])
            w_gate = w_gate.at[:, LANES:LANES + ML_HEADS].set(w_in[:, n_main + ML_HEADS:])
            b_gate = jnp.zeros((1, 2 * LANES), F32)
            b_gate = b_gate.at[0, 0:ML_HEADS].set(b_in[n_main:n_main + ML_HEADS])
            b_gate = b_gate.at[0, LANES:LANES + ML_HEADS].set(b_in[n_main + ML_HEADS:])
            x = _ml_mixer(
                x, row(pre_mix_g[layer]), row(post_mix_g[layer]),
                w_in[:, :n_main].astype(BF16), row(b_in[:n_main]),
                w_gate.astype(BF16), b_gate,
                row(ml_head_g[j]), ml_w_out[j].astype(BF16))
        x = _ffn(
            x.reshape(bsz * seq, d), row(pre_ffn_g[layer]), row(post_ffn_g[layer]),
            ffn_w_gate[layer].astype(BF16), ffn_w_up[layer].astype(BF16),
            ffn_w_down[layer].astype(BF16)).reshape(bsz, seq, d)
    return x
```

```python
import functools

import jax
import jax.numpy as jnp
from jax import lax
from jax.experimental import pallas as pl
from jax.experimental.pallas import tpu as pltpu

F32 = jnp.float32
BF16 = jnp.bfloat16

EPS = 1e-6
LRU_C = 8.0
CONF_KERNEL = 31
LRU_CONV = 4
LRU_HEADS = 8
ML_HEADS = 8
ML_HEAD_QK = 64
ML_HEAD_V = 128

SUBLANES = 8
LANES = 128
VMEM_LIMIT_BYTES = 56 * 1024 * 1024

CONF_HIST = 32
LRU_HIST = SUBLANES
CONV_ROWS = 32

TS_AB = 256
TS_ML = 256
TM_FFN = 512
FFN_CHUNK = 512


def _rms(x, g):
    return x * lax.rsqrt(jnp.mean(x * x, axis=-1, keepdims=True) + EPS) * g


def _dot(a, b):
    return jnp.dot(a, b, preferred_element_type=F32)


def _const_spec(shape):
    zeros = (0,) * len(shape)
    return pl.BlockSpec(shape, lambda *_: zeros)


def _scan_time(x, op, fill):
    n = x.shape[0]
    row = lax.broadcasted_iota(jnp.int32, x.shape, 0)
    s = 1
    while s < n:
        shifted = pltpu.roll(x, s, 0)
        x = op(x, jnp.where(row >= s, shifted, fill))
        s *= 2
    return x


def _ffn_kernel(x_ref, pre_g_ref, post_g_ref, wg_ref, wu_ref, wd_ref, o_ref):
    x = x_ref[...]
    h = _rms(x, pre_g_ref[...]).astype(BF16)
    d_ff = wg_ref.shape[1]
    acc = None
    for c in range(0, d_ff, FFN_CHUNK):
        w = min(FFN_CHUNK, d_ff - c)
        g = _dot(h, wg_ref[:, c:c + w])
        u = _dot(h, wu_ref[:, c:c + w])
        a = (g * jax.nn.sigmoid(g) * u).astype(BF16)
        p = _dot(a, wd_ref[c:c + w, :])
        acc = p if acc is None else acc + p
    o_ref[...] = x + _rms(acc, post_g_ref[...])


def _ffn(x2d, pre_g, post_g, wg, wu, wd):
    t, d = x2d.shape
    d_ff = wg.shape[1]
    return pl.pallas_call(
        _ffn_kernel,
        grid=(t // TM_FFN,),
        in_specs=[
            pl.BlockSpec((TM_FFN, d), lambda i: (i, 0)),
            _const_spec((1, d)), _const_spec((1, d)),
            _const_spec((d, d_ff)), _const_spec((d, d_ff)), _const_spec((d_ff, d)),
        ],
        out_specs=pl.BlockSpec((TM_FFN, d), lambda i: (i, 0)),
        out_shape=jax.ShapeDtypeStruct((t, d), F32),
        compiler_params=pltpu.CompilerParams(
            dimension_semantics=("arbitrary",), vmem_limit_bytes=VMEM_LIMIT_BYTES),
        name="swiglu_ffn",
    )(x2d, pre_g, post_g, wg, wu, wd)


def _ab_kernel(x_ref, pre_g_ref, post_g_ref, w_in_ref, b_in_ref,
               cw_ref, cb_ref, lng_ref, lnb_ref,
               lw_ref, lb_ref, wblk_ref, ba_ref, bx_ref, lam_ref, w_out_ref,
               o_ref, ybuf, ysh, wtap, rbuf, rsh, a_s, b_s, hcar, yab):
    ts = x_ref.shape[0]
    cw = ybuf.shape[1]
    lw = rbuf.shape[1]

    @pl.when(pl.program_id(1) == 0)
    def _():
        ybuf[0:CONF_HIST, :] = jnp.zeros((CONF_HIST, cw), F32)
        rbuf[0:LRU_HIST, :] = jnp.zeros((LRU_HIST, lw), F32)
        hcar[...] = jnp.zeros(hcar.shape, F32)

    x = x_ref[...]
    h = _rms(x, pre_g_ref[...]).astype(BF16)
    u = _dot(h, w_in_ref[...]) + b_in_ref[...]
    ybuf[CONF_HIST:CONF_HIST + ts, :] = u[:, 0:cw] * jax.nn.sigmoid(u[:, cw:2 * cw])
    gate_in = u[:, 2 * cw:2 * cw + lw]
    rbuf[LRU_HIST:LRU_HIST + ts, :] = u[:, 2 * cw + lw:]

    win = ybuf[...]
    for r in range(1, SUBLANES):
        ysh[r - 1, r:r + CONF_HIST + ts, :] = win
    for j in range(CONF_KERNEL):
        wtap[j] = jnp.broadcast_to(cw_ref[j:j + 1, :], (SUBLANES, cw))
    ln_g = lng_ref[...]
    ln_b = lnb_ref[...]
    groups = CONV_ROWS // SUBLANES
    for r0 in range(0, ts, CONV_ROWS):
        accs = [jnp.broadcast_to(cb_ref[...], (SUBLANES, cw))] * groups
        for j in range(CONF_KERNEL):
            off = CONF_HIST - (CONF_KERNEL - 1) + j
            shift = (-off) % SUBLANES
            start = r0 + off + shift
            src = ybuf if shift == 0 else ysh.at[shift - 1]
            wj = wtap[j]
            accs = [accs[g] + wj * src[start + g * SUBLANES:start + (g + 1) * SUBLANES, :]
                    for g in range(groups)]
        acc = jnp.concatenate(accs, axis=0)
        mu = jnp.mean(acc, axis=-1, keepdims=True)
        cen = acc - mu
        var = jnp.mean(cen * cen, axis=-1, keepdims=True)
        yn = cen * lax.rsqrt(var + EPS) * ln_g + ln_b
        yab[r0:r0 + CONV_ROWS, 0:cw] = (yn * jax.nn.sigmoid(yn)).astype(BF16)

    rwin = rbuf[...]
    for r in range(1, LRU_CONV):
        rsh[r - 1, r:r + LRU_HIST + ts, :] = rwin
    xr = jnp.broadcast_to(lb_ref[...], (ts, lw))
    for j in range(LRU_CONV):
        off = LRU_HIST - (LRU_CONV - 1) + j
        shift = (-off) % SUBLANES
        src = rbuf if shift == 0 else rsh.at[shift - 1]
        xr = xr + lw_ref[j:j + 1, :] * src[off + shift:off + shift + ts, :]
    xb = xr.astype(BF16)
    half = lw // 2
    g0 = _dot(xb[:, 0:half], wblk_ref[0])
    g1 = _dot(xb[:, half:], wblk_ref[1])
    r = jax.nn.sigmoid(jnp.concatenate([g0[:, 0:half], g1[:, 0:half]], axis=1) + ba_ref[...])
    i = jax.nn.sigmoid(jnp.concatenate([g0[:, half:], g1[:, half:]], axis=1) + bx_ref[...])
    lam = lam_ref[...]
    log_base = -(jnp.maximum(-lam, 0.0) + jnp.log1p(jnp.exp(-jnp.abs(lam))))
    log_a = (LRU_C * log_base) * r
    a = jnp.exp(log_a)
    z = -jnp.tanh(log_a) * (a * a + 1.0)
    b = jnp.where(z > 0.0, z * lax.rsqrt(z), 0.0) * (i * xr)

    n = ts // SUBLANES
    pitch = a_s.shape[1] // SUBLANES
    nslab = lw // LANES
    for sub in range(SUBLANES):
        for k in range(nslab):
            a_s[k, sub * pitch:sub * pitch + n, :] = a[sub * n:(sub + 1) * n, k * LANES:(k + 1) * LANES]
            b_s[k, sub * pitch:sub * pitch + n, :] = b[sub * n:(sub + 1) * n, k * LANES:(k + 1) * LANES]
    row = lax.broadcasted_iota(jnp.int32, (SUBLANES, LANES), 0)
    for k in range(nslab):
        h_loc, p_loc = [], []
        for g in range(n):
            ag = a_s[k, pl.ds(g, SUBLANES, stride=pitch), :]
            bg = b_s[k, pl.ds(g, SUBLANES, stride=pitch), :]
            h_loc.append(bg if g == 0 else ag * h_loc[-1] + bg)
            p_loc.append(ag if g == 0 else ag * p_loc[-1])
        pa, ph = p_loc[-1], h_loc[-1]
        s = 1
        while s < SUBLANES:
            keep = row >= s
            pa_sh = pltpu.roll(pa, s, 0)
            ph_sh = pltpu.roll(ph, s, 0)
            ph = jnp.where(keep, pa * ph_sh + ph, ph)
            pa = jnp.where(keep, pa * pa_sh, pa)
            s *= 2
        c0 = hcar[:, k * LANES:(k + 1) * LANES]
        c_in = (jnp.where(row >= 1, pltpu.roll(pa, 1, 0), 1.0) * c0
                + jnp.where(row >= 1, pltpu.roll(ph, 1, 0), 0.0))
        c_out = pa[SUBLANES - 1:SUBLANES, :] * c0[0:1, :] + ph[SUBLANES - 1:SUBLANES, :]
        hcar[:, k * LANES:(k + 1) * LANES] = jnp.broadcast_to(c_out, (SUBLANES, LANES))
        for g in range(n):
            b_s[k, pl.ds(g, SUBLANES, stride=pitch), :] = h_loc[g] + p_loc[g] * c_in
    hseq = jnp.concatenate(
        [jnp.concatenate([b_s[k, sub * pitch:sub * pitch + n, :] for k in range(nslab)], axis=1)
         for sub in range(SUBLANES)], axis=0)
    yab[:, cw:cw + lw] = (hseq * jax.nn.gelu(gate_in)).astype(BF16)

    ybuf[0:CONF_HIST, :] = ybuf[ts:ts + CONF_HIST, :]
    rbuf[0:LRU_HIST, :] = rbuf[ts:ts + LRU_HIST, :]

    y = _dot(yab[...], w_out_ref[...])
    o_ref[...] = x + _rms(y, post_g_ref[...])


def _ab_mixer(x, pre_g, post_g, w_in, b_in, conf_w, conf_b, ln_g, ln_b,
              lru_w, lru_b, wblk, ba, bx, lam, w_out):
    bsz, seq, d = x.shape
    cw = conf_w.shape[1]
    lw = lru_w.shape[1]
    n_in = w_in.shape[1]
    xspec = pl.BlockSpec((None, TS_AB, d), lambda b, s: (b, s, 0))
    return pl.pallas_call(
        _ab_kernel,
        grid=(bsz, seq // TS_AB),
        in_specs=[
            xspec,
            _const_spec((1, d)), _const_spec((1, d)),
            _const_spec((d, n_in)), _const_spec((1, n_in)),
            _const_spec((CONF_KERNEL, cw)), _const_spec((1, cw)), _const_spec((1, cw)), _const_spec((1, cw)),
            _const_spec((LRU_CONV, lw)), _const_spec((1, lw)),
            _const_spec(wblk.shape), _const_spec((1, lw)), _const_spec((1, lw)), _const_spec((1, lw)),
            _const_spec((cw + lw, d)),
        ],
        out_specs=xspec,
        out_shape=jax.ShapeDtypeStruct(x.shape, F32),
        scratch_shapes=[
            pltpu.VMEM((CONF_HIST + TS_AB, cw), F32),
            pltpu.VMEM((SUBLANES - 1, CONF_HIST + TS_AB + SUBLANES, cw), F32),
            pltpu.VMEM((CONF_KERNEL, SUBLANES, cw), F32),
            pltpu.VMEM((LRU_HIST + TS_AB, lw), F32),
            pltpu.VMEM((LRU_CONV - 1, LRU_HIST + TS_AB + SUBLANES, lw), F32),
            pltpu.VMEM((lw // LANES, SUBLANES * (TS_AB // SUBLANES + SUBLANES), LANES), F32),
            pltpu.VMEM((lw // LANES, SUBLANES * (TS_AB // SUBLANES + SUBLANES), LANES), F32),
            pltpu.VMEM((SUBLANES, lw), F32),
            pltpu.VMEM((TS_AB, cw + lw), BF16),
        ],
        compiler_params=pltpu.CompilerParams(
            dimension_semantics=("arbitrary", "arbitrary"), vmem_limit_bytes=VMEM_LIMIT_BYTES),
        name="conformer_rglru_mixer",
    )(x, pre_g, post_g, w_in, b_in, conf_w, conf_b, ln_g, ln_b,
      lru_w, lru_b, wblk, ba, bx, lam, w_out)


def _ml_kernel(x_ref, pre_g_ref, post_g_ref, w_main_ref, b_main_ref, w_gate_ref, b_gate_ref,
               head_g_ref, w_out_ref, o_ref, c_ref, m_ref, hbuf):
    ts = x_ref.shape[0]
    qk = ML_HEADS * ML_HEAD_QK
    vd = ML_HEADS * ML_HEAD_V

    @pl.when(pl.program_id(1) == 0)
    def _():
        c_ref[...] = jnp.zeros(c_ref.shape, F32)
        m_ref[...] = jnp.zeros(m_ref.shape, F32)

    x = x_ref[...]
    h = _rms(x, pre_g_ref[...]).astype(BF16)
    u = _dot(h, w_main_ref[...]) + b_main_ref[...]
    gates = _dot(h, w_gate_ref[...]) + b_gate_ref[...]
    ig = gates[:, 0:LANES]
    fg = gates[:, LANES:2 * LANES]
    lf = jnp.minimum(fg, 0.0) - jnp.log1p(jnp.exp(-jnp.abs(fg)))

    bcum = _scan_time(lf, jnp.add, 0.0)
    rs = ig - bcum
    m_prev = m_ref[0:1, :]
    log_inter = bcum + m_prev
    m_row = jnp.maximum(log_inter, bcum + _scan_time(rs, jnp.maximum, -jnp.inf))
    c_col = bcum - m_row
    inter_w = jnp.exp(log_inter - m_row)
    e_negm = jnp.exp(-m_row)
    b_last = bcum[ts - 1:ts, :]
    log_w = b_last - bcum + ig
    m_new = jnp.maximum(b_last + m_prev, jnp.max(log_w, axis=0, keepdims=True))
    w_s = jnp.exp(log_w - m_new)
    decay = jnp.exp(b_last + m_prev - m_new)
    m_ref[...] = jnp.broadcast_to(m_new, m_ref.shape)
    rs_t = jnp.transpose(rs)

    causal = (lax.broadcasted_iota(jnp.int32, (ts, ts), 0)
              >= lax.broadcasted_iota(jnp.int32, (ts, ts), 1))
    lane_head = lax.broadcasted_iota(jnp.int32, (ts, LANES), 1) // ML_HEAD_QK
    ones_blk = jnp.ones((ts, LANES), BF16)
    head_g = head_g_ref[...]

    for hd in range(ML_HEADS):
        pair = (hd // 2) * LANES
        q_p = u[:, pair:pair + LANES].astype(BF16)
        k_m = jnp.where(lane_head == (hd % 2), u[:, qk + pair:qk + pair + LANES], 0.0) * (ML_HEAD_QK ** -0.5)
        v_h = u[:, 2 * qk + hd * ML_HEAD_V:2 * qk + (hd + 1) * ML_HEAD_V]
        o_h = u[:, 2 * qk + vd + hd * ML_HEAD_V:2 * qk + vd + (hd + 1) * ML_HEAD_V]
        v_ext = jnp.concatenate([v_h.astype(BF16), ones_blk], axis=1)

        scores = lax.dot_general(q_p, k_m.astype(BF16), (((1,), (1,)), ((), ())),
                                 preferred_element_type=F32)
        d_w = jnp.exp(jnp.where(causal, c_col[:, hd:hd + 1] + rs_t[hd:hd + 1, :], -jnp.inf))
        p = (scores * d_w).astype(BF16)
        c_h = c_ref[hd]
        res = _dot(p, v_ext) + inter_w[:, hd:hd + 1] * _dot(q_p, c_h.astype(BF16))
        num = res[:, 0:ML_HEAD_V]
        den = res[:, ML_HEAD_V:]
        hh = num / jnp.maximum(jnp.abs(den), e_negm[:, hd:hd + 1])
        hn = hh * lax.rsqrt(jnp.mean(hh * hh, axis=-1, keepdims=True) + EPS)
        hn = hn * head_g[:, hd * ML_HEAD_V:(hd + 1) * ML_HEAD_V]
        hbuf[:, hd * ML_HEAD_V:(hd + 1) * ML_HEAD_V] = (jax.nn.sigmoid(o_h) * hn).astype(BF16)

        kw = (k_m * w_s[:, hd:hd + 1]).astype(BF16)
        upd = lax.dot_general(kw, v_ext, (((0,), (0,)), ((), ())), preferred_element_type=F32)
        c_ref[hd] = decay[:, hd:hd + 1] * c_h + upd

    y = _dot(hbuf[...], w_out_ref[...])
    o_ref[...] = x + _rms(y, post_g_ref[...])


def _ml_mixer(x, pre_g, post_g, w_main, b_main, w_gate, b_gate, head_g, w_out):
    bsz, seq, d = x.shape
    n_main = w_main.shape[1]
    vd = w_out.shape[0]
    xspec = pl.BlockSpec((None, TS_ML, d), lambda b, s: (b, s, 0))
    return pl.pallas_call(
        _ml_kernel,
        grid=(bsz, seq // TS_ML),
        in_specs=[
            xspec,
            _const_spec((1, d)), _const_spec((1, d)),
            _const_spec((d, n_main)), _const_spec((1, n_main)),
            _const_spec((d, 2 * LANES)), _const_spec((1, 2 * LANES)),
            _const_spec((1, vd)), _const_spec((vd, d)),
        ],
        out_specs=xspec,
        out_shape=jax.ShapeDtypeStruct(x.shape, F32),
        scratch_shapes=[
            pltpu.VMEM((ML_HEADS, LANES, 2 * ML_HEAD_V), F32),
            pltpu.VMEM((SUBLANES, LANES), F32),
            pltpu.VMEM((TS_ML, vd), BF16),
        ],
        compiler_params=pltpu.CompilerParams(
            dimension_semantics=("arbitrary", "arbitrary"), vmem_limit_bytes=VMEM_LIMIT_BYTES),
        name="mlstm_mixer",
    )(x, pre_g, post_g, w_main, b_main, w_gate, b_gate, head_g, w_out)


def _block_diag(w):
    n, d, e = w.shape
    eye = jnp.eye(n, dtype=w.dtype)
    return jnp.einsum('hde,hg->hdge', w, eye).reshape(n * d, n * e)


def kernel(x, pre_mix_g, post_mix_g, pre_ffn_g, post_ffn_g, ab_w_in, ab_b_in, conf_conv_w, conf_conv_b, conf_ln_g, conf_ln_b, lru_conv_w, lru_conv_b, lru_w_a, lru_b_a, lru_w_x, lru_b_x, lru_lambda, ab_w_out, ml_w_in, ml_b_in, ml_head_g, ml_w_out, ffn_w_gate, ffn_w_up, ffn_w_down):
    bsz, seq, d = x.shape
    depth = pre_mix_g.shape[0]
    row = lambda v: v.reshape(1, -1).astype(F32)

    for layer in range(depth):
        j = layer // 2
        if layer % 2 == 0:
            grp = LRU_HEADS // 2
            wa, wx = lru_w_a[j], lru_w_x[j]
            wblk = jnp.stack([
                jnp.concatenate([_block_diag(wa[g * grp:(g + 1) * grp]),
                                 _block_diag(wx[g * grp:(g + 1) * grp])], axis=1)
                for g in range(2)]).astype(BF16)
            x = _ab_mixer(
                x, row(pre_mix_g[layer]), row(post_mix_g[layer]),
                ab_w_in[j].astype(BF16), row(ab_b_in[j]),
                conf_conv_w[j], row(conf_conv_b[j]), row(conf_ln_g[j]), row(conf_ln_b[j]),
                lru_conv_w[j], row(lru_conv_b[j]), wblk,
                row(lru_b_a[j]), row(lru_b_x[j]), row(lru_lambda[j]),
                ab_w_out[j].astype(BF16))
        else:
            n_main = 2 * ML_HEADS * ML_HEAD_QK + 2 * ML_HEADS * ML_HEAD_V
            w_in, b_in = ml_w_in[j], ml_b_in[j]
            w_gate = jnp.zeros((d, 2 * LANES), F32)
            w_gate = w_gate.at[:, 0:ML_HEADS].set(w_in[:, n_main:n_main + ML_HEADS])
            w_gate = w_gate.at[:, LANES:LANES + ML_HEADS].set(w_in[:, n_main + ML_HEADS:])
            b_gate = jnp.zeros((1, 2 * LANES), F32)
            b_gate = b_gate.at[0, 0:ML_HEADS].set(b_in[n_main:n_main + ML_HEADS])
            b_gate = b_gate.at[0, LANES:LANES + ML_HEADS].set(b_in[n_main + ML_HEADS:])
            x = _ml_mixer(
                x, row(pre_mix_g[layer]), row(post_mix_g[layer]),
                w_in[:, :n_main].astype(BF16), row(b_in[:n_main]),
                w_gate.astype(BF16), b_gate,
                row(ml_head_g[j]), ml_w_out[j].astype(BF16))
        x = _ffn(
            x.reshape(bsz * seq, d), row(pre_ffn_g[layer]), row(post_ffn_g[layer]),
            ffn_w_gate[layer].astype(BF16), ffn_w_up[layer].astype(BF16),
            ffn_w_down[layer].astype(BF16)).reshape(bsz, seq, d)
    return x
```

```python
import jax
import jax.numpy as jnp
from jax import lax
from jax.experimental import pallas as pl
from jax.experimental.pallas import tpu as pltpu

F32 = jnp.float32
BF16 = jnp.bfloat16

EPS = 1e-6
LRU_C = 8.0
CONF_KERNEL = 31
LRU_CONV = 4
LRU_HEADS = 8
ML_HEADS = 8
ML_HEAD_QK = 64
ML_HEAD_V = 128

SUBLANES = 8
LANES = 128
VMEM_LIMIT_BYTES = 56 * 1024 * 1024

CONF_HIST = 32
LRU_HIST = SUBLANES
CONV_ROWS = 32

TS_AB = 512
TS_ML = 512
ML_CHUNK = 256
TM_FFN = 1024
FFN_CHUNK = 512


def _rms(x, g):
    return x * lax.rsqrt(jnp.mean(x * x, axis=-1, keepdims=True) + EPS) * g


def _dot(a, b):
    return jnp.dot(a, b, preferred_element_type=F32)


def _const_spec(shape, single_buffer=False):
    zeros = (0,) * len(shape)
    if single_buffer:
        return pl.BlockSpec(shape, lambda *_: zeros, pipeline_mode=pl.Buffered(1))
    return pl.BlockSpec(shape, lambda *_: zeros)


def _scan_time(x, op, fill):
    n = x.shape[0]
    row = lax.broadcasted_iota(jnp.int32, x.shape, 0)
    s = 1
    while s < n:
        shifted = pltpu.roll(x, s, 0)
        x = op(x, jnp.where(row >= s, shifted, fill))
        s *= 2
    return x


def _ffn_kernel(x_ref, pre_g_ref, post_g_ref, wg_ref, wu_ref, wd_ref, o_ref):
    x = x_ref[...]
    h = _rms(x, pre_g_ref[...]).astype(BF16)
    d_ff = wg_ref.shape[1]
    acc = None
    for c in range(0, d_ff, FFN_CHUNK):
        w = min(FFN_CHUNK, d_ff - c)
        g = _dot(h, wg_ref[:, c:c + w])
        u = _dot(h, wu_ref[:, c:c + w])
        a = (g * jax.nn.sigmoid(g) * u).astype(BF16)
        p = _dot(a, wd_ref[c:c + w, :])
        acc = p if acc is None else acc + p
    o_ref[...] = x + _rms(acc, post_g_ref[...])


def _ffn(x2d, pre_g, post_g, wg, wu, wd):
    t, d = x2d.shape
    d_ff = wg.shape[1]
    return pl.pallas_call(
        _ffn_kernel,
        grid=(t // TM_FFN,),
        in_specs=[
            pl.BlockSpec((TM_FFN, d), lambda i: (i, 0)),
            _const_spec((1, d)), _const_spec((1, d)),
            _const_spec((d, d_ff), True), _const_spec((d, d_ff), True), _const_spec((d_ff, d), True),
        ],
        out_specs=pl.BlockSpec((TM_FFN, d), lambda i: (i, 0)),
        out_shape=jax.ShapeDtypeStruct((t, d), F32),
        compiler_params=pltpu.CompilerParams(
            dimension_semantics=("arbitrary",), vmem_limit_bytes=VMEM_LIMIT_BYTES),
        name="swiglu_ffn",
    )(x2d, pre_g, post_g, wg, wu, wd)


def _ab_kernel(x_ref, pre_g_ref, post_g_ref, w_in_ref, b_in_ref,
               cw_ref, cb_ref, lng_ref, lnb_ref,
               lw_ref, lb_ref, wblk_ref, ba_ref, bx_ref, lam_ref, w_out_ref,
               o_ref, ybuf, ysh, wtap, rbuf, rsh, a_s, b_s, hcar, yab):
    ts = x_ref.shape[0]
    cw = ybuf.shape[1]
    lw = rbuf.shape[1]

    @pl.when(pl.program_id(1) == 0)
    def _():
        ybuf[0:CONF_HIST, :] = jnp.zeros((CONF_HIST, cw), F32)
        rbuf[0:LRU_HIST, :] = jnp.zeros((LRU_HIST, lw), F32)
        hcar[...] = jnp.zeros(hcar.shape, F32)

    x = x_ref[...]
    h = _rms(x, pre_g_ref[...]).astype(BF16)
    u = _dot(h, w_in_ref[...]) + b_in_ref[...]
    ybuf[CONF_HIST:CONF_HIST + ts, :] = u[:, 0:cw] * jax.nn.sigmoid(u[:, cw:2 * cw])
    gate_in = u[:, 2 * cw:2 * cw + lw]
    rbuf[LRU_HIST:LRU_HIST + ts, :] = u[:, 2 * cw + lw:]

    win = ybuf[...]
    for r in range(1, SUBLANES):
        ysh[r - 1, r:r + CONF_HIST + ts, :] = win
    for j in range(CONF_KERNEL):
        wtap[j] = jnp.broadcast_to(cw_ref[j:j + 1, :], (SUBLANES, cw))
    ln_g = lng_ref[...]
    ln_b = lnb_ref[...]
    groups = CONV_ROWS // SUBLANES
    for r0 in range(0, ts, CONV_ROWS):
        accs = [jnp.broadcast_to(cb_ref[...], (SUBLANES, cw))] * groups
        for j in range(CONF_KERNEL):
            off = CONF_HIST - (CONF_KERNEL - 1) + j
            shift = (-off) % SUBLANES
            start = r0 + off + shift
            src = ybuf if shift == 0 else ysh.at[shift - 1]
            wj = wtap[j]
            accs = [accs[g] + wj * src[start + g * SUBLANES:start + (g + 1) * SUBLANES, :]
                    for g in range(groups)]
        acc = jnp.concatenate(accs, axis=0)
        mu = jnp.mean(acc, axis=-1, keepdims=True)
        cen = acc - mu
        var = jnp.mean(cen * cen, axis=-1, keepdims=True)
        yn = cen * lax.rsqrt(var + EPS) * ln_g + ln_b
        yab[r0:r0 + CONV_ROWS, 0:cw] = (yn * jax.nn.sigmoid(yn)).astype(BF16)

    rwin = rbuf[...]
    for r in range(1, LRU_CONV):
        rsh[r - 1, r:r + LRU_HIST + ts, :] = rwin
    xr = jnp.broadcast_to(lb_ref[...], (ts, lw))
    for j in range(LRU_CONV):
        off = LRU_HIST - (LRU_CONV - 1) + j
        shift = (-off) % SUBLANES
        src = rbuf if shift == 0 else rsh.at[shift - 1]
        xr = xr + lw_ref[j:j + 1, :] * src[off + shift:off + shift + ts, :]
    xb = xr.astype(BF16)
    half = lw // 2
    g0 = _dot(xb[:, 0:half], wblk_ref[0])
    g1 = _dot(xb[:, half:], wblk_ref[1])
    r = jax.nn.sigmoid(jnp.concatenate([g0[:, 0:half], g1[:, 0:half]], axis=1) + ba_ref[...])
    i = jax.nn.sigmoid(jnp.concatenate([g0[:, half:], g1[:, half:]], axis=1) + bx_ref[...])
    lam = lam_ref[...]
    log_base = -(jnp.maximum(-lam, 0.0) + jnp.log1p(jnp.exp(-jnp.abs(lam))))
    log_a = (LRU_C * log_base) * r
    a = jnp.exp(log_a)
    z = -jnp.tanh(log_a) * (a * a + 1.0)
    b = jnp.where(z > 0.0, z * lax.rsqrt(z), 0.0) * (i * xr)

    n = ts // SUBLANES
    pitch = a_s.shape[1] // SUBLANES
    nslab = lw // LANES
    for sub in range(SUBLANES):
        for k in range(nslab):
            a_s[k, sub * pitch:sub * pitch + n, :] = a[sub * n:(sub + 1) * n, k * LANES:(k + 1) * LANES]
            b_s[k, sub * pitch:sub * pitch + n, :] = b[sub * n:(sub + 1) * n, k * LANES:(k + 1) * LANES]
    row = lax.broadcasted_iota(jnp.int32, (SUBLANES, LANES), 0)
    for k in range(nslab):
        h_loc, p_loc = [], []
        for g in range(n):
            ag = a_s[k, pl.ds(g, SUBLANES, stride=pitch), :]
            bg = b_s[k, pl.ds(g, SUBLANES, stride=pitch), :]
            h_loc.append(bg if g == 0 else ag * h_loc[-1] + bg)
            p_loc.append(ag if g == 0 else ag * p_loc[-1])
        pa, ph = p_loc[-1], h_loc[-1]
        s = 1
        while s < SUBLANES:
            keep = row >= s
            pa_sh = pltpu.roll(pa, s, 0)
            ph_sh = pltpu.roll(ph, s, 0)
            ph = jnp.where(keep, pa * ph_sh + ph, ph)
            pa = jnp.where(keep, pa * pa_sh, pa)
            s *= 2
        c0 = hcar[:, k * LANES:(k + 1) * LANES]
        c_in = (jnp.where(row >= 1, pltpu.roll(pa, 1, 0), 1.0) * c0
                + jnp.where(row >= 1, pltpu.roll(ph, 1, 0), 0.0))
        c_out = pa[SUBLANES - 1:SUBLANES, :] * c0[0:1, :] + ph[SUBLANES - 1:SUBLANES, :]
        hcar[:, k * LANES:(k + 1) * LANES] = jnp.broadcast_to(c_out, (SUBLANES, LANES))
        for g in range(n):
            b_s[k, pl.ds(g, SUBLANES, stride=pitch), :] = h_loc[g] + p_loc[g] * c_in
    hseq = jnp.concatenate(
        [jnp.concatenate([b_s[k, sub * pitch:sub * pitch + n, :] for k in range(nslab)], axis=1)
         for sub in range(SUBLANES)], axis=0)
    yab[:, cw:cw + lw] = (hseq * jax.nn.gelu(gate_in)).astype(BF16)

    ybuf[0:CONF_HIST, :] = ybuf[ts:ts + CONF_HIST, :]
    rbuf[0:LRU_HIST, :] = rbuf[ts:ts + LRU_HIST, :]

    y = _dot(yab[...], w_out_ref[...])
    o_ref[...] = x + _rms(y, post_g_ref[...])


def _ab_mixer(x, pre_g, post_g, w_in, b_in, conf_w, conf_b, ln_g, ln_b,
              lru_w, lru_b, wblk, ba, bx, lam, w_out):
    bsz, seq, d = x.shape
    cw = conf_w.shape[1]
    lw = lru_w.shape[1]
    n_in = w_in.shape[1]
    xspec = pl.BlockSpec((None, TS_AB, d), lambda b, s: (b, s, 0))
    return pl.pallas_call(
        _ab_kernel,
        grid=(bsz, seq // TS_AB),
        in_specs=[
            xspec,
            _const_spec((1, d)), _const_spec((1, d)),
            _const_spec((d, n_in)), _const_spec((1, n_in)),
            _const_spec((CONF_KERNEL, cw)), _const_spec((1, cw)), _const_spec((1, cw)), _const_spec((1, cw)),
            _const_spec((LRU_CONV, lw)), _const_spec((1, lw)),
            _const_spec(wblk.shape), _const_spec((1, lw)), _const_spec((1, lw)), _const_spec((1, lw)),
            _const_spec((cw + lw, d)),
        ],
        out_specs=xspec,
        out_shape=jax.ShapeDtypeStruct(x.shape, F32),
        scratch_shapes=[
            pltpu.VMEM((CONF_HIST + TS_AB, cw), F32),
            pltpu.VMEM((SUBLANES - 1, CONF_HIST + TS_AB + SUBLANES, cw), F32),
            pltpu.VMEM((CONF_KERNEL, SUBLANES, cw), F32),
            pltpu.VMEM((LRU_HIST + TS_AB, lw), F32),
            pltpu.VMEM((LRU_CONV - 1, LRU_HIST + TS_AB + SUBLANES, lw), F32),
            pltpu.VMEM((lw // LANES, SUBLANES * (TS_AB // SUBLANES + SUBLANES), LANES), F32),
            pltpu.VMEM((lw // LANES, SUBLANES * (TS_AB // SUBLANES + SUBLANES), LANES), F32),
            pltpu.VMEM((SUBLANES, lw), F32),
            pltpu.VMEM((TS_AB, cw + lw), BF16),
        ],
        compiler_params=pltpu.CompilerParams(
            dimension_semantics=("arbitrary", "arbitrary"), vmem_limit_bytes=VMEM_LIMIT_BYTES),
        name="conformer_rglru_mixer",
    )(x, pre_g, post_g, w_in, b_in, conf_w, conf_b, ln_g, ln_b,
      lru_w, lru_b, wblk, ba, bx, lam, w_out)


def _ml_kernel(x_ref, pre_g_ref, post_g_ref, w_main_ref, b_main_ref, w_gate_ref, b_gate_ref,
               head_g_ref, w_out_ref, o_ref, c_ref, m_ref, hbuf):
    ts = x_ref.shape[0]
    qk = ML_HEADS * ML_HEAD_QK
    vd = ML_HEADS * ML_HEAD_V
    cl = ML_CHUNK

    @pl.when(pl.program_id(1) == 0)
    def _():
        c_ref[...] = jnp.zeros(c_ref.shape, F32)
        m_ref[...] = jnp.zeros(m_ref.shape, F32)

    x = x_ref[...]
    h = _rms(x, pre_g_ref[...]).astype(BF16)
    gates = _dot(h, w_gate_ref[...]) + b_gate_ref[...]
    u = _dot(h, w_main_ref[...]) + b_main_ref[...]

    causal = (lax.broadcasted_iota(jnp.int32, (cl, cl), 0)
              >= lax.broadcasted_iota(jnp.int32, (cl, cl), 1))
    lane_head = lax.broadcasted_iota(jnp.int32, (cl, LANES), 1) // ML_HEAD_QK
    ones_blk = jnp.ones((cl, LANES), BF16)
    head_g = head_g_ref[...]

    for c0 in range(0, ts, cl):
        ig = gates[c0:c0 + cl, 0:LANES]
        fg = gates[c0:c0 + cl, LANES:2 * LANES]
        lf = jnp.minimum(fg, 0.0) - jnp.log1p(jnp.exp(-jnp.abs(fg)))

        bcum = _scan_time(lf, jnp.add, 0.0)
        rs = ig - bcum
        m_prev = m_ref[0:1, :]
        log_inter = bcum + m_prev
        m_row = jnp.maximum(log_inter, bcum + _scan_time(rs, jnp.maximum, -jnp.inf))
        c_col = bcum - m_row
        inter_w = jnp.exp(log_inter - m_row)
        e_negm = jnp.exp(-m_row)
        b_last = bcum[cl - 1:cl, :]
        log_w = b_last - bcum + ig
        m_new = jnp.maximum(b_last + m_prev, jnp.max(log_w, axis=0, keepdims=True))
        w_s = jnp.exp(log_w - m_new)
        decay = jnp.exp(b_last + m_prev - m_new)
        m_ref[...] = jnp.broadcast_to(m_new, m_ref.shape)
        rs_t = jnp.transpose(rs)

        for hd in range(ML_HEADS):
            pair = (hd // 2) * LANES
            q_p = u[c0:c0 + cl, pair:pair + LANES].astype(BF16)
            k_m = jnp.where(lane_head == (hd % 2), u[c0:c0 + cl, qk + pair:qk + pair + LANES], 0.0)
            k_m = k_m * (ML_HEAD_QK ** -0.5)
            v_h = u[c0:c0 + cl, 2 * qk + hd * ML_HEAD_V:2 * qk + (hd + 1) * ML_HEAD_V]
            o_h = u[c0:c0 + cl, 2 * qk + vd + hd * ML_HEAD_V:2 * qk + vd + (hd + 1) * ML_HEAD_V]
            v_ext = jnp.concatenate([v_h.astype(BF16), ones_blk], axis=1)

            scores = lax.dot_general(q_p, k_m.astype(BF16), (((1,), (1,)), ((), ())),
                                     preferred_element_type=F32)
            d_w = jnp.exp(jnp.where(causal, c_col[:, hd:hd + 1] + rs_t[hd:hd + 1, :], -jnp.inf))
            p = (scores * d_w).astype(BF16)
            c_h = c_ref[hd]
            res = _dot(p, v_ext) + inter_w[:, hd:hd + 1] * _dot(q_p, c_h.astype(BF16))
            num = res[:, 0:ML_HEAD_V]
            den = res[:, ML_HEAD_V:]
            hh = num / jnp.maximum(jnp.abs(den), e_negm[:, hd:hd + 1])
            hn = hh * lax.rsqrt(jnp.mean(hh * hh, axis=-1, keepdims=True) + EPS)
            hn = hn * head_g[:, hd * ML_HEAD_V:(hd + 1) * ML_HEAD_V]
            hbuf[c0:c0 + cl, hd * ML_HEAD_V:(hd + 1) * ML_HEAD_V] = (jax.nn.sigmoid(o_h) * hn).astype(BF16)

            kw = (k_m * w_s[:, hd:hd + 1]).astype(BF16)
            upd = lax.dot_general(kw, v_ext, (((0,), (0,)), ((), ())), preferred_element_type=F32)
            c_ref[hd] = decay[:, hd:hd + 1] * c_h + upd

    y = _dot(hbuf[...], w_out_ref[...])
    o_ref[...] = x + _rms(y, post_g_ref[...])


def _ml_mixer(x, pre_g, post_g, w_main, b_main, w_gate, b_gate, head_g, w_out):
    bsz, seq, d = x.shape
    n_main = w_main.shape[1]
    vd = w_out.shape[0]
    xspec = pl.BlockSpec((None, TS_ML, d), lambda b, s: (b, s, 0))
    return pl.pallas_call(
        _ml_kernel,
        grid=(bsz, seq // TS_ML),
        in_specs=[
            xspec,
            _const_spec((1, d)), _const_spec((1, d)),
            _const_spec((d, n_main)), _const_spec((1, n_main)),
            _const_spec((d, 2 * LANES)), _const_spec((1, 2 * LANES)),
            _const_spec((1, vd)), _const_spec((vd, d)),
        ],
        out_specs=xspec,
        out_shape=jax.ShapeDtypeStruct(x.shape, F32),
        scratch_shapes=[
            pltpu.VMEM((ML_HEADS, LANES, 2 * ML_HEAD_V), F32),
            pltpu.VMEM((SUBLANES, LANES), F32),
            pltpu.VMEM((TS_ML, vd), BF16),
        ],
        compiler_params=pltpu.CompilerParams(
            dimension_semantics=("arbitrary", "arbitrary"), vmem_limit_bytes=VMEM_LIMIT_BYTES),
        name="mlstm_mixer",
    )(x, pre_g, post_g, w_main, b_main, w_gate, b_gate, head_g, w_out)


def _block_diag(w):
    n, d, e = w.shape
    eye = jnp.eye(n, dtype=w.dtype)
    return jnp.einsum('hde,hg->hdge', w, eye).reshape(n * d, n * e)


def kernel(x, pre_mix_g, post_mix_g, pre_ffn_g, post_ffn_g, ab_w_in, ab_b_in, conf_conv_w, conf_conv_b, conf_ln_g, conf_ln_b, lru_conv_w, lru_conv_b, lru_w_a, lru_b_a, lru_w_x, lru_b_x, lru_lambda, ab_w_out, ml_w_in, ml_b_in, ml_head_g, ml_w_out, ffn_w_gate, ffn_w_up, ffn_w_down):
    bsz, seq, d = x.shape
    depth = pre_mix_g.shape[0]
    row = lambda v: v.reshape(1, -1).astype(F32)

    for layer in range(depth):
        j = layer // 2
        if layer % 2 == 0:
            grp = LRU_HEADS // 2
            wa, wx = lru_w_a[j], lru_w_x[j]
            wblk = jnp.stack([
                jnp.concatenate([_block_diag(wa[g * grp:(g + 1) * grp]),
                                 _block_diag(wx[g * grp:(g + 1) * grp])], axis=1)
                for g in range(2)]).astype(BF16)
            x = _ab_mixer(
                x, row(pre_mix_g[layer]), row(post_mix_g[layer]),
                ab_w_in[j].astype(BF16), row(ab_b_in[j]),
                conf_conv_w[j], row(conf_conv_b[j]), row(conf_ln_g[j]), row(conf_ln_b[j]),
                lru_conv_w[j], row(lru_conv_b[j]), wblk,
                row(lru_b_a[j]), row(lru_b_x[j]), row(lru_lambda[j]),
                ab_w_out[j].astype(BF16))
        else:
            n_main = 2 * ML_HEADS * ML_HEAD_QK + 2 * ML_HEADS * ML_HEAD_V
            w_in, b_in = ml_w_in[j], ml_b_in[j]
            w_gate = jnp.zeros((d, 2 * LANES), F32)
            w_gate = w_gate.at[:, 0:ML_HEADS].set(w_in[:, n_main:n_main + ML_HEADS])
            w_gate = w_gate.at[:, LANES:LANES + ML_HEADS].set(w_in[:, n_main + ML_HEADS:])
            b_gate = jnp.zeros((1, 2 * LANES), F32)
            b_gate = b_gate.at[0, 0:ML_HEADS].set(b_in[n_main:n_main + ML_HEADS])
            b_gate = b_gate.at[0, LANES:LANES + ML_HEADS].set(b_in[n_main + ML_HEADS:])
            x = _ml_mixer(
                x, row(pre_mix_g[layer]), row(post_mix_g[layer]),
                w_in[:, :n_main].astype(BF16), row(b_in[:n_main]),
                w_gate.astype(BF16), b_gate,
                row(ml_head_g[j]), ml_w_out[j].astype(BF16))
        x = _ffn(
            x.reshape(bsz * seq, d), row(pre_ffn_g[layer]), row(post_ffn_g[layer]),
            ffn_w_gate[layer].astype(BF16), ffn_w_up[layer].astype(BF16),
            ffn_w_down[layer].astype(BF16)).reshape(bsz, seq, d)
    return x
```
